```python
import math
import jax, jax.numpy as jnp
from jax import lax
import numpy as np

D_MODEL = 1024
BATCH = 2
SEQ = 16384
DEPTH = 2
DEC_BATCH = 16
DEC_SEQ = 64
PAST_LEN = 4096

CHUNK = 64
N_A_LAYERS = DEPTH // 2
N_B_LAYERS = DEPTH - N_A_LAYERS
H_A = 8
HD_A = 64
T5_BUCKETS = 32
T5_MAX_DIST = 128
H_B = 16
HD_B = 64
BAND_CHUNKS = 8
B_PAST = BAND_CHUNKS * CHUNK
REL_CLIP = 256
N_GROUPS = 4
EXPERTS_PER_GROUP = 8
N_EXPERTS = N_GROUPS * EXPERTS_PER_GROUP
TOP_K_IN_GROUP = 2
D_EXPERT = 512
Q_BLOCK = 128
EPS = 1e-6
NEG = -1e30

kernel_name = 'streaming_diffattn_chunkband_hmoe'

F32 = jnp.float32


def rms_norm(x, g):
    xf = x.astype(F32)
    y = xf * lax.rsqrt(jnp.mean(xf * xf, axis=-1, keepdims=True) + EPS)
    return (y * g.astype(F32)).astype(x.dtype)


def t5_bucket(rel):
    nb = T5_BUCKETS // 2
    max_exact = nb // 2
    ret = (rel > 0).astype(jnp.int32) * nb
    n = jnp.abs(rel)
    nf = jnp.maximum(n, 1).astype(F32)
    large = max_exact + (jnp.log(nf / max_exact) / math.log(T5_MAX_DIST / max_exact)
                         * (nb - max_exact)).astype(jnp.int32)
    large = jnp.minimum(large, nb - 1)
    return ret + jnp.where(n < max_exact, n, large)


def diff_lambda(lv, lam_init):
    lv = lv.astype(F32)
    return jnp.exp(jnp.sum(lv[0] * lv[1])) - jnp.exp(jnp.sum(lv[2] * lv[3])) + lam_init


def diff_qkv(h, w_qkv, g_q, g_k):
    b, s, _ = h.shape
    q, k, v = jnp.split(h @ w_qkv, 3, axis=-1)
    q = rms_norm(q.reshape(b, s, H_A, 2, HD_A), g_q)
    k = rms_norm(k.reshape(b, s, H_A, 2, HD_A), g_k)
    v = v.reshape(b, s, H_A, 2 * HD_A)
    return q, k, v


def diff_attend(q, k, v, qpos, kpos, t5_table, lam):
    logits = jnp.einsum('bqhcd,bkhcd->bchqk', q, k, preferred_element_type=F32) * (HD_A ** -0.5)
    bias = jnp.moveaxis(t5_table[t5_bucket(kpos[None, :] - qpos[:, None])].astype(F32), -1, 0)
    mask = (kpos[None, :] // CHUNK) <= (qpos[:, None] // CHUNK)
    logits = jnp.where(mask, logits + bias, NEG)
    p = jax.nn.softmax(logits, axis=-1)
    a = p[:, 0] - lam * p[:, 1]
    return jnp.einsum('bhqk,bkhe->bqhe', a.astype(v.dtype), v)


def diff_prompt(q, k, v, t5_table, lam):
    b, s = q.shape[:2]
    kpos = jnp.arange(s)

    def block(i):
        start = i * Q_BLOCK
        q_blk = lax.dynamic_slice_in_dim(q, start, Q_BLOCK, axis=1)
        qpos = start + jnp.arange(Q_BLOCK)
        return diff_attend(q_blk, k, v, qpos, kpos, t5_table, lam)

    out = lax.map(block, jnp.arange(s // Q_BLOCK))
    return jnp.moveaxis(out, 0, 1).reshape(b, s, H_A, 2 * HD_A)


def diff_sample(q, k, v, cache_k, cache_v, t5_table, lam):
    db, sd = q.shape[:2]
    past = cache_k.shape[1]
    k_all = jnp.concatenate([cache_k.reshape(db, past, H_A, 2, HD_A).astype(k.dtype), k], axis=1)
    v_all = jnp.concatenate([cache_v.astype(v.dtype), v], axis=1)
    qpos = past + jnp.arange(sd)
    kpos = jnp.arange(past + sd)
    return diff_attend(q, k_all, v_all, qpos, kpos, t5_table, lam)


def diff_out(o, g_sub, w_o, lam_init):
    b, s = o.shape[:2]
    o = rms_norm(o, g_sub) * (1.0 - lam_init)
    return o.reshape(b, s, H_A * 2 * HD_A) @ w_o


def shared_kv(x, g_norm, w_kv, g_k):
    b, s, _ = x.shape
    k, v = jnp.split(rms_norm(x, g_norm) @ w_kv, 2, axis=-1)
    k = rms_norm(k.reshape(b, s, H_B, HD_B), g_k)
    return k, v.reshape(b, s, H_B, HD_B)


def band_query(h, w_q, g_q):
    b, s, _ = h.shape
    return rms_norm((h @ w_q).reshape(b, s, H_B, HD_B), g_q)


def band_attend(q, k, v, qpos, kpos, rel_table):
    logits = jnp.einsum('bqhd,bkhd->bhqk', q, k, preferred_element_type=F32) * (HD_B ** -0.5)
    rel = jnp.clip(qpos[:, None] - kpos[None, :], -REL_CLIP, REL_CLIP) + REL_CLIP
    bias = jnp.moveaxis(rel_table[rel].astype(F32), -1, 0)
    dc = qpos[:, None] // CHUNK - kpos[None, :] // CHUNK
    mask = (dc >= 0) & (dc <= BAND_CHUNKS) & (kpos[None, :] >= 0)
    p = jax.nn.softmax(jnp.where(mask, logits + bias, NEG), axis=-1)
    return jnp.einsum('bhqk,bkhd->bqhd', p.astype(v.dtype), v)


def band_prompt(q, k, v, rel_table):
    b, s = q.shape[:2]
    pad = ((0, 0), (B_PAST, 0), (0, 0), (0, 0))
    kpad, vpad = jnp.pad(k, pad), jnp.pad(v, pad)
    band = B_PAST + CHUNK

    def chunk(c):
        start = c * CHUNK
        q_c = lax.dynamic_slice_in_dim(q, start, CHUNK, axis=1)
        k_c = lax.dynamic_slice_in_dim(kpad, start, band, axis=1)
        v_c = lax.dynamic_slice_in_dim(vpad, start, band, axis=1)
        qpos = start + jnp.arange(CHUNK)
        kpos = start - B_PAST + jnp.arange(band)
        return band_attend(q_c, k_c, v_c, qpos, kpos, rel_table)

    out = lax.map(chunk, jnp.arange(s // CHUNK))
    return jnp.moveaxis(out, 0, 1).reshape(b, s, H_B * HD_B)


def band_sample(q, k, v, cache_k, cache_v, past, rel_table):
    db, sd = q.shape[:2]
    w = cache_k.shape[1]
    k_all = jnp.concatenate([cache_k.astype(k.dtype), k], axis=1)
    v_all = jnp.concatenate([cache_v.astype(v.dtype), v], axis=1)
    qpos = past + jnp.arange(sd)
    kpos = jnp.concatenate([past - w + jnp.arange(w), qpos])
    return band_attend(q, k_all, v_all, qpos, kpos, rel_table).reshape(db, sd, H_B * HD_B)


def hier_moe(h, w_grp, b_grp, w_exp, b_exp, w_gate, w_up, w_down):
    shp = h.shape
    hf = h.reshape(-1, shp[-1])
    g_prob = jax.nn.softmax(jnp.einsum('nd,dg->ng', hf, w_grp, preferred_element_type=F32)
                            + b_grp.astype(F32), axis=-1)
    g_val, g_sel = lax.top_k(g_prob, 1)
    e_logits = (jnp.einsum('nd,de->ne', hf, w_exp, preferred_element_type=F32)
                + b_exp.astype(F32)).reshape(-1, N_GROUPS, EXPERTS_PER_GROUP)
    e_in = jnp.take_along_axis(e_logits, g_sel[:, :, None], axis=1)[:, 0]
    e_val, e_idx = lax.top_k(e_in, TOP_K_IN_GROUP)
    gates = jax.nn.softmax(e_val, axis=-1) * g_val
    eid = g_sel * EXPERTS_PER_GROUP + e_idx
    combine = jnp.einsum('nk,nke->ne', gates, jax.nn.one_hot(eid, N_EXPERTS, dtype=F32))
    out = jnp.zeros(hf.shape, F32)
    for e in range(N_EXPERTS):
        y = (jax.nn.silu(hf @ w_gate[e]) * (hf @ w_up[e])) @ w_down[e]
        out = out + combine[:, e:e + 1] * y.astype(F32)
    return out.astype(h.dtype).reshape(shp)


def setup_inputs(seed: int = 0) -> dict:
    key = jax.random.key(seed)
    ks = jax.random.split(key, 32)
    b_win = min(B_PAST, PAST_LEN)
    e2 = 2 * HD_A

    def nrm(k, shape, scale=1.0):
        return jax.random.normal(k, shape, F32) * scale

    return {
        'x_prompt': nrm(ks[0], (BATCH, SEQ, D_MODEL)),
        'x_sample': nrm(ks[1], (DEC_BATCH, DEC_SEQ, D_MODEL)),
        'cache_a_k': nrm(ks[2], (N_A_LAYERS, DEC_BATCH, PAST_LEN, H_A, e2)),
        'cache_a_v': nrm(ks[3], (N_A_LAYERS, DEC_BATCH, PAST_LEN, H_A, e2)),
        'cache_b_k': nrm(ks[4], (DEC_BATCH, b_win, H_B, HD_B)),
        'cache_b_v': nrm(ks[5], (DEC_BATCH, b_win, H_B, HD_B)),
        'g_mix': 1.0 + nrm(ks[6], (DEPTH, D_MODEL), 0.02),
        'g_ffn': 1.0 + nrm(ks[7], (DEPTH, D_MODEL), 0.02),
        'a_w_qkv': nrm(ks[8], (N_A_LAYERS, D_MODEL, 3 * H_A * e2), D_MODEL ** -0.5),
        'a_g_q': 1.0 + nrm(ks[9], (N_A_LAYERS, 2, HD_A), 0.02),
        'a_g_k': 1.0 + nrm(ks[10], (N_A_LAYERS, 2, HD_A), 0.02),
        'a_lam': nrm(ks[11], (N_A_LAYERS, 4, HD_A), 0.1),
        'a_g_sub': 1.0 + nrm(ks[12], (N_A_LAYERS, e2), 0.02),
        'a_w_o': nrm(ks[13], (N_A_LAYERS, H_A * e2, D_MODEL), (H_A * e2) ** -0.5),
        't5_table': nrm(ks[14], (T5_BUCKETS, H_A), 0.5),
        'kv_g_norm': 1.0 + nrm(ks[15], (D_MODEL,), 0.02),
        'kv_w': nrm(ks[16], (D_MODEL, 2 * H_B * HD_B), D_MODEL ** -0.5),
        'kv_g_k': 1.0 + nrm(ks[17], (HD_B,), 0.02),
        'b_w_q': nrm(ks[18], (N_B_LAYERS, D_MODEL, H_B * HD_B), D_MODEL ** -0.5),
        'b_g_q': 1.0 + nrm(ks[19], (N_B_LAYERS, HD_B), 0.02),
        'b_rel': nrm(ks[20], (N_B_LAYERS, 2 * REL_CLIP + 1, H_B), 0.5),
        'b_w_o': nrm(ks[21], (N_B_LAYERS, H_B * HD_B, D_MODEL), (H_B * HD_B) ** -0.5),
        'moe_w_grp': nrm(ks[22], (DEPTH, D_MODEL, N_GROUPS), D_MODEL ** -0.5),
        'moe_b_grp': nrm(ks[23], (DEPTH, N_GROUPS), 0.01),
        'moe_w_exp': nrm(ks[24], (DEPTH, D_MODEL, N_EXPERTS), D_MODEL ** -0.5),
        'moe_b_exp': nrm(ks[25], (DEPTH, N_EXPERTS), 0.01),
        'moe_w_gate': nrm(ks[26], (DEPTH, N_EXPERTS, D_MODEL, D_EXPERT), D_MODEL ** -0.5),
        'moe_w_up': nrm(ks[27], (DEPTH, N_EXPERTS, D_MODEL, D_EXPERT), D_MODEL ** -0.5),
        'moe_w_down': nrm(ks[28], (DEPTH, N_EXPERTS, D_EXPERT, D_MODEL), D_EXPERT ** -0.5),
    }


def reference(x_prompt, x_sample, cache_a_k, cache_a_v, cache_b_k, cache_b_v, g_mix, g_ffn,
              a_w_qkv, a_g_q, a_g_k, a_lam, a_g_sub, a_w_o, t5_table, kv_g_norm, kv_w, kv_g_k,
              b_w_q, b_g_q, b_rel, b_w_o, moe_w_grp, moe_b_grp, moe_w_exp, moe_b_exp,
              moe_w_gate, moe_w_up, moe_w_down):
    past = cache_a_k.shape[2]
    seq = x_prompt.shape[1]
    bp, db = x_prompt.shape[0], x_sample.shape[0]
    xp, xs = x_prompt, x_sample
    a_kp, a_vp, a_ks, a_vs = [], [], [], []
    for i in range(DEPTH):
        hp = rms_norm(xp, g_mix[i])
        hs = rms_norm(xs, g_mix[i])
        if i < N_A_LAYERS:
            lam_init = 0.8 - 0.6 * math.exp(-0.3 * i)
            lam = diff_lambda(a_lam[i], lam_init)
            qp, kp, vp = diff_qkv(hp, a_w_qkv[i], a_g_q[i], a_g_k[i])
            qs, ks, vs = diff_qkv(hs, a_w_qkv[i], a_g_q[i], a_g_k[i])
            op = diff_prompt(qp, kp, vp, t5_table, lam)
            os_ = diff_sample(qs, ks, vs, cache_a_k[i], cache_a_v[i], t5_table, lam)
            mp = diff_out(op, a_g_sub[i], a_w_o[i], lam_init)
            ms = diff_out(os_, a_g_sub[i], a_w_o[i], lam_init)
            a_kp.append(kp.reshape(bp, seq, H_A, 2 * HD_A))
            a_vp.append(vp)
            a_ks.append(ks.reshape(db, ks.shape[1], H_A, 2 * HD_A))
            a_vs.append(vs)
        else:
            j = i - N_A_LAYERS
            if j == 0:
                kb_p, vb_p = shared_kv(xp, kv_g_norm, kv_w, kv_g_k)
                kb_s, vb_s = shared_kv(xs, kv_g_norm, kv_w, kv_g_k)
            qbp = band_query(hp, b_w_q[j], b_g_q[j])
            qbs = band_query(hs, b_w_q[j], b_g_q[j])
            mp = band_prompt(qbp, kb_p, vb_p, b_rel[j]) @ b_w_o[j]
            ms = band_sample(qbs, kb_s, vb_s, cache_b_k, cache_b_v, past, b_rel[j]) @ b_w_o[j]
        xp = xp + mp.astype(xp.dtype)
        xs = xs + ms.astype(xs.dtype)
        xp = xp + hier_moe(rms_norm(xp, g_ffn[i]), moe_w_grp[i], moe_b_grp[i], moe_w_exp[i],
                           moe_b_exp[i], moe_w_gate[i], moe_w_up[i], moe_w_down[i])
        xs = xs + hier_moe(rms_norm(xs, g_ffn[i]), moe_w_grp[i], moe_b_grp[i], moe_w_exp[i],
                           moe_b_exp[i], moe_w_gate[i], moe_w_up[i], moe_w_down[i])
    keep = min(B_PAST, seq)
    new_a_k_prompt = jnp.stack(a_kp)
    new_a_v_prompt = jnp.stack(a_vp)
    new_a_k_sample = jnp.stack(a_ks)
    new_a_v_sample = jnp.stack(a_vs)
    new_b_k_prompt = kb_p[:, seq - keep:]
    new_b_v_prompt = vb_p[:, seq - keep:]
    return (xp, xs, new_a_k_prompt, new_a_v_prompt, new_a_k_sample, new_a_v_sample,
            new_b_k_prompt, new_b_v_prompt, kb_s, vb_s)
```

```python
import functools
import math

import jax
import jax.numpy as jnp
from jax import lax
from jax.experimental import pallas as pl
from jax.experimental.pallas import tpu as pltpu

F32 = jnp.float32
BF16 = jnp.bfloat16

CHUNK = 64
H_A = 8
HD_A = 64
T5_BUCKETS = 32
T5_MAX_DIST = 128
H_B = 16
HD_B = 64
BAND_CHUNKS = 8
B_PAST = BAND_CHUNKS * CHUNK
REL_CLIP = 256
N_GROUPS = 4
EXPERTS_PER_GROUP = 8
N_EXPERTS = N_GROUPS * EXPERTS_PER_GROUP
EPS = 1e-6
NEG = -1e30
LOG2E = math.log2(math.e)

LANES = 128
MXU_DIM = 256
VMEM_LIMIT = 56 * 1024 * 1024

TOK_TILE = 512
ATT_TILE = 512
MOE_TILE = 1024
SAMPLE_KEY_TILE = 1024


def _cparams(sem):
    return pltpu.CompilerParams(dimension_semantics=sem, vmem_limit_bytes=VMEM_LIMIT)


def _dot(a, b):
    return jnp.dot(a, b, preferred_element_type=F32)


def _dot_nt(a, b):
    return lax.dot_general(a, b, (((1,), (1,)), ((), ())), preferred_element_type=F32)


def _rms_rows(x):
    return x * lax.rsqrt(jnp.mean(x * x, axis=-1, keepdims=True) + EPS)


def _seg_norm(y, bd, gain):
    outs = []
    for j in range(y.shape[1] // MXU_DIM):
        blk = y[:, j * MXU_DIM:(j + 1) * MXU_DIM]
        ms = _dot((blk * blk).astype(BF16), bd)
        outs.append(blk * lax.rsqrt(ms + EPS) * gain[:, j * MXU_DIM:(j + 1) * MXU_DIM])
    return jnp.concatenate(outs, axis=1)


def _qkv_a_kernel(x_ref, g_ref, w_ref, bd_ref, gq_ref, gk_ref,
                  q_ref, k_ref, v_ref, kb_ref, vb_ref, *, q_scale):
    d = x_ref.shape[1]
    h = (_rms_rows(x_ref[...]) * g_ref[...]).astype(BF16)
    bd = bd_ref[...]
    yq = _dot(h, w_ref[:, 0:d])
    q_ref[...] = (_seg_norm(yq, bd, gq_ref[...]) * q_scale).astype(BF16)
    yk = _dot(h, w_ref[:, d:2 * d])
    k = _seg_norm(yk, bd, gk_ref[...])
    k_ref[...] = k
    kb_ref[...] = k.astype(BF16)
    v = _dot(h, w_ref[:, 2 * d:3 * d])
    v_ref[...] = v
    vb_ref[...] = v.astype(BF16)


def _qkv_a(x, g, w, bd, gq, gk, q_scale):
    n, d = x.shape
    t = min(TOK_TILE, n)
    row = lambda i: (i, 0)
    fixed = lambda i: (0, 0)
    tok = pl.BlockSpec((t, d), row)
    return pl.pallas_call(
        functools.partial(_qkv_a_kernel, q_scale=q_scale),
        grid=(n // t,),
        in_specs=[tok, pl.BlockSpec((1, d), fixed), pl.BlockSpec((d, 3 * d), fixed),
                  pl.BlockSpec((MXU_DIM, MXU_DIM), fixed),
                  pl.BlockSpec((1, d), fixed), pl.BlockSpec((1, d), fixed)],
        out_specs=[tok, tok, tok, tok, tok],
        out_shape=[jax.ShapeDtypeStruct((n, d), BF16), jax.ShapeDtypeStruct((n, d), F32),
                   jax.ShapeDtypeStruct((n, d), F32), jax.ShapeDtypeStruct((n, d), BF16),
                   jax.ShapeDtypeStruct((n, d), BF16)],
        compiler_params=_cparams(("arbitrary",)),
        name="qkv_a",
    )(x, g, w, bd, gq, gk)


def _proj_b_kernel(x_ref, gm_ref, gkv_ref, wq_ref, wkv_ref, bd_ref, gq_ref, gk_ref,
                   q_ref, k_ref, v_ref, kb_ref, vb_ref, *, q_scale):
    d = x_ref.shape[1]
    xn = _rms_rows(x_ref[...])
    bd = bd_ref[...]
    hq = (xn * gm_ref[...]).astype(BF16)
    yq = _dot(hq, wq_ref[...])
    q_ref[...] = (_seg_norm(yq, bd, gq_ref[...]) * q_scale).astype(BF16)
    hk = (xn * gkv_ref[...]).astype(BF16)
    yk = _dot(hk, wkv_ref[:, 0:d])
    k = _seg_norm(yk, bd, gk_ref[...])
    k_ref[...] = k
    kb_ref[...] = k.astype(BF16)
    v = _dot(hk, wkv_ref[:, d:2 * d])
    v_ref[...] = v
    vb_ref[...] = v.astype(BF16)


def _proj_b(x, gm, gkv, wq, wkv, bd, gq, gk, q_scale):
    n, d = x.shape
    t = min(TOK_TILE, n)
    row = lambda i: (i, 0)
    fixed = lambda i: (0, 0)
    tok = pl.BlockSpec((t, d), row)
    vec = pl.BlockSpec((1, d), fixed)
    return pl.pallas_call(
        functools.partial(_proj_b_kernel, q_scale=q_scale),
        grid=(n // t,),
        in_specs=[tok, vec, vec, pl.BlockSpec((d, d), fixed), pl.BlockSpec((d, 2 * d), fixed),
                  pl.BlockSpec((MXU_DIM, MXU_DIM), fixed), vec, vec],
        out_specs=[tok, tok, tok, tok, tok],
        out_shape=[jax.ShapeDtypeStruct((n, d), BF16), jax.ShapeDtypeStruct((n, d), F32),
                   jax.ShapeDtypeStruct((n, d), F32), jax.ShapeDtypeStruct((n, d), BF16),
                   jax.ShapeDtypeStruct((n, d), BF16)],
        compiler_params=_cparams(("arbitrary",)),
        name="proj_b",
    )(x, gm, gkv, wq, wkv, bd, gq, gk)


def _split_components(q):
    lane = lax.broadcasted_iota(jnp.int32, q.shape, 1)
    zero = jnp.zeros_like(q)
    return jnp.concatenate([jnp.where(lane < HD_A, q, zero), jnp.where(lane >= HD_A, q, zero)], axis=0)


def _diff_lambda(lam_ref, lam_init):
    lv = lam_ref[...]
    s1 = jnp.sum(lv[0:1] * lv[1:2], axis=-1, keepdims=True)
    s2 = jnp.sum(lv[2:3] * lv[3:4], axis=-1, keepdims=True)
    return jnp.exp(s1) - jnp.exp(s2) + lam_init


def _online_update(z, v, m_ref, l_ref, acc_ref):
    m_old = m_ref[...]
    m_new = jnp.maximum(m_old, jnp.max(z, axis=-1, keepdims=True))
    alpha = jnp.exp2(m_old - m_new)
    p = jnp.exp2(z - m_new)
    l_ref[...] = alpha * l_ref[...] + jnp.sum(p, axis=-1, keepdims=True)
    acc_ref[...] = alpha * acc_ref[...] + _dot(p.astype(BF16), v)
    m_ref[...] = m_new


def _diff_finish(acc, l, lam, gsub, t, out_scale):
    o = acc[0:t] / l[0:t] - lam * (acc[t:2 * t] / l[t:2 * t])
    return (_rms_rows(o) * gsub * out_scale)


def _flash_a_kernel(lam_ref, q_ref, k_ref, v_ref, bdiag_ref, bsub_ref, cfar_ref, gsub_ref,
                    o_ref, qq_sc, m_sc, l_sc, acc_sc, *, lam_init):
    t = q_ref.shape[0]
    i = pl.program_id(2)
    qq_sc[...] = _split_components(q_ref[...])
    m_sc[...] = jnp.full(m_sc.shape, NEG, F32)
    l_sc[...] = jnp.zeros(l_sc.shape, F32)
    acc_sc[...] = jnp.zeros(acc_sc.shape, F32)
    cfar = cfar_ref[0][:, 0:1]

    def block(j):
        off = pl.multiple_of(j * t, t)
        s = _dot_nt(qq_sc[...], k_ref[pl.ds(off, t), :])
        return s, v_ref[pl.ds(off, t), :]

    def far_body(j, carry):
        s, v = block(j)
        _online_update(s + cfar, v, m_sc, l_sc, acc_sc)
        return carry

    lax.fori_loop(0, jnp.maximum(i - 1, 0), far_body, 0)

    @pl.when(i >= 1)
    def _():
        s, v = block(i - 1)
        z = (s.reshape(2, t, t) + bsub_ref[0][None]).reshape(2 * t, t)
        _online_update(z, v, m_sc, l_sc, acc_sc)

    s, v = block(i)
    z = (s.reshape(2, t, t) + bdiag_ref[0][None]).reshape(2 * t, t)
    _online_update(z, v, m_sc, l_sc, acc_sc)

    lam = _diff_lambda(lam_ref, lam_init)
    o_ref[...] = _diff_finish(acc_sc[...], l_sc[...], lam, gsub_ref[...], t,
                              1.0 - lam_init).astype(BF16)


def _flash_a(lam_p, q, kb, vb, bdiag, bsub, cfar, gsub, batch, lam_init):
    n, d = q.shape
    s = n // batch
    t = min(ATT_TILE, s)
    nq = s // t
    return pl.pallas_call(
        functools.partial(_flash_a_kernel, lam_init=lam_init),
        grid=(batch, H_A, nq),
        in_specs=[pl.BlockSpec(lam_p.shape, lambda b, h, i: (0, 0)),
                  pl.BlockSpec((t, LANES), lambda b, h, i: (b * nq + i, h)),
                  pl.BlockSpec((s, LANES), lambda b, h, i: (b, h)),
                  pl.BlockSpec((s, LANES), lambda b, h, i: (b, h)),
                  pl.BlockSpec((1, t, t), lambda b, h, i: (h, 0, 0)),
                  pl.BlockSpec((1, t, t), lambda b, h, i: (h, 0, 0)),
                  pl.BlockSpec((1, 1, LANES), lambda b, h, i: (h, 0, 0)),
                  pl.BlockSpec((1, LANES), lambda b, h, i: (0, 0))],
        out_specs=pl.BlockSpec((t, LANES), lambda b, h, i: (b * nq + i, h)),
        out_shape=jax.ShapeDtypeStruct((n, d), BF16),
        scratch_shapes=[pltpu.VMEM((2 * t, LANES), BF16), pltpu.VMEM((2 * t, 1), F32),
                        pltpu.VMEM((2 * t, 1), F32), pltpu.VMEM((2 * t, LANES), F32)],
        compiler_params=_cparams(("arbitrary", "arbitrary", "arbitrary")),
        name="flash_a",
    )(lam_p, q, kb, vb, bdiag, bsub, cfar, gsub)


def _sample_a_kernel(lam_ref, q_ref, kn_ref, vn_ref, kc_ref, vc_ref, bc_ref, bn_ref, gsub_ref,
                     o_ref, m_sc, l_sc, acc_sc, *, lam_init):
    sd = q_ref.shape[0]
    kb = pl.program_id(1)

    @pl.when(kb == 0)
    def _():
        m_sc[...] = jnp.full(m_sc.shape, NEG, F32)
        l_sc[...] = jnp.zeros(l_sc.shape, F32)
        acc_sc[...] = jnp.zeros(acc_sc.shape, F32)

    def head_q(h):
        return _split_components(q_ref[:, h * LANES:(h + 1) * LANES])

    for h in range(H_A):
        cols = slice(h * LANES, (h + 1) * LANES)
        kh = kc_ref[0, :, cols].astype(BF16)
        vh = vc_ref[0, :, cols].astype(BF16)
        s = _dot_nt(head_q(h), kh)
        z = (s.reshape(2, sd, -1) + bc_ref[h][None]).reshape(2 * sd, -1)
        _online_update(z, vh, m_sc.at[h], l_sc.at[h], acc_sc.at[h])

    @pl.when(kb == pl.num_programs(1) - 1)
    def _():
        lam = _diff_lambda(lam_ref, lam_init)
        for h in range(H_A):
            cols = slice(h * LANES, (h + 1) * LANES)
            s = _dot_nt(head_q(h), kn_ref[:, cols])
            z = (s.reshape(2, sd, -1) + bn_ref[h][None]).reshape(2 * sd, -1)
            _online_update(z, vn_ref[:, cols], m_sc.at[h], l_sc.at[h], acc_sc.at[h])
            o_ref[:, cols] = _diff_finish(acc_sc[h], l_sc[h], lam, gsub_ref[...], sd,
                                          1.0 - lam_init).astype(BF16)


def _sample_a(lam_p, q, kn, vn, kc, vc, bc, bn, gsub, lam_init):
    db, past, d = kc.shape
    sd = q.shape[0] // db
    tk = min(SAMPLE_KEY_TILE, past)
    return pl.pallas_call(
        functools.partial(_sample_a_kernel, lam_init=lam_init),
        grid=(db, past // tk),
        in_specs=[pl.BlockSpec(lam_p.shape, lambda b, j: (0, 0)),
                  pl.BlockSpec((sd, d), lambda b, j: (b, 0)),
                  pl.BlockSpec((sd, d), lambda b, j: (b, 0)),
                  pl.BlockSpec((sd, d), lambda b, j: (b, 0)),
                  pl.BlockSpec((1, tk, d), lambda b, j: (b, j, 0)),
                  pl.BlockSpec((1, tk, d), lambda b, j: (b, j, 0)),
                  pl.BlockSpec((H_A, sd, tk), lambda b, j: (0, 0, j)),
                  pl.BlockSpec((H_A, sd, sd), lambda b, j: (0, 0, 0)),
                  pl.BlockSpec((1, LANES), lambda b, j: (0, 0))],
        out_specs=pl.BlockSpec((sd, d), lambda b, j: (b, 0)),
        out_shape=jax.ShapeDtypeStruct(q.shape, BF16),
        scratch_shapes=[pltpu.VMEM((H_A, 2 * sd, 1), F32), pltpu.VMEM((H_A, 2 * sd, 1), F32),
                        pltpu.VMEM((H_A, 2 * sd, LANES), F32)],
        compiler_params=_cparams(("arbitrary", "arbitrary")),
        name="sample_a",
    )(lam_p, q, kn, vn, kc, vc, bc, bn, gsub)


def _split_heads(q):
    lane = lax.broadcasted_iota(jnp.int32, q.shape, 1)
    zero = jnp.zeros_like(q)
    return jnp.concatenate([jnp.where(lane < HD_B, q, zero), jnp.where(lane >= HD_B, q, zero)], axis=0)


def _merge_heads(acc, t):
    lane = lax.broadcasted_iota(jnp.int32, (t, LANES), 1)
    return jnp.where(lane < HD_B, acc[0:t], acc[t:2 * t])


def _band_p_kernel(q_ref, k_ref, v_ref, bias_ref, o_ref):
    t = q_ref.shape[0]
    i = pl.program_id(2)
    qq = _split_heads(q_ref[...])
    prev = pl.multiple_of(jnp.maximum(i - 1, 0) * t, t)
    own = pl.multiple_of(i * t, t)
    pen = jnp.where(i == 0, NEG, 0.0).astype(F32)
    bias = bias_ref[...].reshape(2 * t, 2 * t)
    z0 = _dot_nt(qq, k_ref[pl.ds(prev, t), :]) + bias[:, 0:t] + pen
    z1 = _dot_nt(qq, k_ref[pl.ds(own, t), :]) + bias[:, t:2 * t]
    m = jnp.maximum(jnp.max(z0, axis=-1, keepdims=True), jnp.max(z1, axis=-1, keepdims=True))
    p0 = jnp.exp2(z0 - m)
    p1 = jnp.exp2(z1 - m)
    l = jnp.sum(p0, axis=-1, keepdims=True) + jnp.sum(p1, axis=-1, keepdims=True)
    acc = _dot(p0.astype(BF16), v_ref[pl.ds(prev, t), :]) + _dot(p1.astype(BF16), v_ref[pl.ds(own, t), :])
    o_ref[...] = _merge_heads(acc / l, t).astype(BF16)


def _band_p(q, kb, vb, bias, batch):
    n, d = q.shape
    s = n // batch
    t = min(ATT_TILE, s)
    nq = s // t
    return pl.pallas_call(
        _band_p_kernel,
        grid=(batch, H_B // 2, nq),
        in_specs=[pl.BlockSpec((t, LANES), lambda b, h, i: (b * nq + i, h)),
                  pl.BlockSpec((s, LANES), lambda b, h, i: (b, h)),
                  pl.BlockSpec((s, LANES), lambda b, h, i: (b, h)),
                  pl.BlockSpec((2, t, 2 * t), lambda b, h, i: (h, 0, 0))],
        out_specs=pl.BlockSpec((t, LANES), lambda b, h, i: (b * nq + i, h)),
        out_shape=jax.ShapeDtypeStruct((n, d), BF16),
        compiler_params=_cparams(("arbitrary", "arbitrary", "arbitrary")),
        name="band_p",
    )(q, kb, vb, bias)


def _band_s_kernel(q_ref, kn_ref, vn_ref, kc_ref, vc_ref, bias_ref, o_ref):
    sd = q_ref.shape[0]
    for hp in range(H_B // 2):
        cols = slice(hp * LANES, (hp + 1) * LANES)
        qq = _split_heads(q_ref[:, cols])
        k = jnp.concatenate([kc_ref[0, :, cols].astype(BF16), kn_ref[:, cols]], axis=0)
        v = jnp.concatenate([vc_ref[0, :, cols].astype(BF16), vn_ref[:, cols]], axis=0)
        z = _dot_nt(qq, k) + bias_ref[2 * hp:2 * hp + 2].reshape(2 * sd, -1)
        m = jnp.max(z, axis=-1, keepdims=True)
        p = jnp.exp2(z - m)
        l = jnp.sum(p, axis=-1, keepdims=True)
        acc = _dot(p.astype(BF16), v)
        o_ref[:, cols] = _merge_heads(acc / l, sd).astype(BF16)


def _band_s(q, kn, vn, kc, vc, bias):
    db, w, d = kc.shape
    sd = q.shape[0] // db
    tokb = pl.BlockSpec((sd, d), lambda b: (b, 0))
    cache = pl.BlockSpec((1, w, d), lambda b: (b, 0, 0))
    return pl.pallas_call(
        _band_s_kernel,
        grid=(db,),
        in_specs=[tokb, tokb, tokb, cache, cache, pl.BlockSpec(bias.shape, lambda b: (0, 0, 0))],
        out_specs=tokb,
        out_shape=jax.ShapeDtypeStruct(q.shape, BF16),
        compiler_params=_cparams(("arbitrary",)),
        name="band_s",
    )(q, kn, vn, kc, vc, bias)


def _out_router_kernel(o_ref, wo_ref, x_ref, g_ref, wr_hi_ref, wr_lo_ref, br_ref,
                       x1_ref, h_ref, comb_ref):
    x1 = x_ref[...] + _dot(o_ref[...], wo_ref[...])
    x1_ref[...] = x1
    h = _rms_rows(x1) * g_ref[...]
    h_hi = h.astype(BF16)
    h_ref[...] = h_hi
    h_lo = (h - h_hi.astype(F32)).astype(BF16)
    logits = (_dot(h_hi, wr_hi_ref[...]) + _dot(h_lo, wr_hi_ref[...])
              + _dot(h_hi, wr_lo_ref[...]) + br_ref[...])
    lg = logits[:, 0:LANES]
    le = logits[:, LANES:2 * LANES]
    lane = lax.broadcasted_iota(jnp.int32, lg.shape, 1)
    big = jnp.int32(LANES)
    lg = jnp.where(lane < N_GROUPS, lg, NEG)
    gmax = jnp.max(lg, axis=-1, keepdims=True)
    gsum = jnp.sum(jnp.exp(lg - gmax), axis=-1, keepdims=True)
    g_val = 1.0 / gsum
    g_sel = jnp.min(jnp.where(lg == gmax, lane, big), axis=-1, keepdims=True)
    in_grp = (lane >= g_sel * EXPERTS_PER_GROUP) & (lane < (g_sel + 1) * EXPERTS_PER_GROUP)
    le = jnp.where(in_grp, le, NEG)
    e1 = jnp.max(le, axis=-1, keepdims=True)
    i1 = jnp.min(jnp.where(le == e1, lane, big), axis=-1, keepdims=True)
    le2 = jnp.where(lane == i1, NEG, le)
    e2 = jnp.max(le2, axis=-1, keepdims=True)
    i2 = jnp.min(jnp.where(le2 == e2, lane, big), axis=-1, keepdims=True)
    r = jnp.exp(e2 - e1)
    w1 = g_val / (1.0 + r)
    w2 = g_val * r / (1.0 + r)
    comb_ref[...] = jnp.where(lane == i1, w1, 0.0) + jnp.where(lane == i2, w2, 0.0)


def _out_router(o, wo, x, g, wr_hi, wr_lo, br):
    n, d = x.shape
    t = min(TOK_TILE, n)
    row = lambda i: (i, 0)
    fixed = lambda i: (0, 0)
    tok = pl.BlockSpec((t, d), row)
    return pl.pallas_call(
        _out_router_kernel,
        grid=(n // t,),
        in_specs=[tok, pl.BlockSpec((d, d), fixed), tok, pl.BlockSpec((1, d), fixed),
                  pl.BlockSpec((d, 2 * LANES), fixed), pl.BlockSpec((d, 2 * LANES), fixed),
                  pl.BlockSpec((1, 2 * LANES), fixed)],
        out_specs=[tok, tok, pl.BlockSpec((t, LANES), row)],
        out_shape=[jax.ShapeDtypeStruct((n, d), F32), jax.ShapeDtypeStruct((n, d), BF16),
                   jax.ShapeDtypeStruct((n, LANES), F32)],
        compiler_params=_cparams(("arbitrary",)),
        name="out_router",
    )(o, wo, x, g, wr_hi, wr_lo, br)


def _moe_kernel(h_ref, comb_ref, wg_ref, wu_ref, wd_ref, x_ref, y_ref, acc_sc):
    e = pl.program_id(1)

    @pl.when(e == 0)
    def _():
        acc_sc[...] = jnp.zeros(acc_sc.shape, F32)

    h = h_ref[...]
    a = _dot(h, wg_ref[0])
    b = _dot(h, wu_ref[0])
    act = (a / (1.0 + jnp.exp(-a))) * b
    y = _dot(act.astype(BF16), wd_ref[0])
    lane = lax.broadcasted_iota(jnp.int32, comb_ref.shape, 1)
    c = jnp.sum(jnp.where(lane == e, comb_ref[...], 0.0), axis=-1, keepdims=True)
    acc_sc[...] += c * y

    @pl.when(e == pl.num_programs(1) - 1)
    def _():
        y_ref[...] = x_ref[...] + acc_sc[...]


def _moe(h, comb, wg, wu, wd, x):
    n, d = x.shape
    ne, _, de = wg.shape
    t = min(MOE_TILE, n)
    row = lambda i, e: (i, 0)
    return pl.pallas_call(
        _moe_kernel,
        grid=(n // t, ne),
        in_specs=[pl.BlockSpec((t, d), row), pl.BlockSpec((t, LANES), row),
                  pl.BlockSpec((1, d, de), lambda i, e: (e, 0, 0)),
                  pl.BlockSpec((1, d, de), lambda i, e: (e, 0, 0)),
                  pl.BlockSpec((1, de, d), lambda i, e: (e, 0, 0)),
                  pl.BlockSpec((t, d), row)],
        out_specs=pl.BlockSpec((t, d), row),
        out_shape=jax.ShapeDtypeStruct((n, d), F32),
        scratch_shapes=[pltpu.VMEM((t, d), F32)],
        compiler_params=_cparams(("arbitrary", "arbitrary")),
        name="moe",
    )(h, comb, wg, wu, wd, x)


def _t5_bucket(rel):
    nb = T5_BUCKETS // 2
    max_exact = nb // 2
    ret = (rel > 0).astype(jnp.int32) * nb
    n = jnp.abs(rel)
    nf = jnp.maximum(n, 1).astype(F32)
    large = max_exact + (jnp.log(nf / max_exact) / math.log(T5_MAX_DIST / max_exact)
                         * (nb - max_exact)).astype(jnp.int32)
    large = jnp.minimum(large, nb - 1)
    return ret + jnp.where(n < max_exact, n, large)


def _t5_bias(t5_table, qpos, kpos):
    bias = jnp.moveaxis(t5_table[_t5_bucket(kpos[None, :] - qpos[:, None])].astype(F32), -1, 0)
    mask = (kpos[None, :] // CHUNK) <= (qpos[:, None] // CHUNK)
    return jnp.where(mask[None], bias * LOG2E, NEG)


def _band_bias(rel_table, qpos, kpos):
    rel = jnp.clip(qpos[:, None] - kpos[None, :], -REL_CLIP, REL_CLIP) + REL_CLIP
    bias = jnp.moveaxis(rel_table[rel].astype(F32), -1, 0)
    dc = qpos[:, None] // CHUNK - kpos[None, :] // CHUNK
    mask = (dc >= 0) & (dc <= BAND_CHUNKS) & (kpos[None, :] >= 0)
    return jnp.where(mask[None], bias * LOG2E, NEG)


def _router_weights(w_grp, b_grp, w_exp, b_exp):
    d = w_grp.shape[0]
    w = jnp.zeros((d, 2 * LANES), F32)
    w = w.at[:, 0:N_GROUPS].set(w_grp).at[:, LANES:LANES + N_EXPERTS].set(w_exp)
    b = jnp.zeros((1, 2 * LANES), F32)
    b = b.at[0, 0:N_GROUPS].set(b_grp).at[0, LANES:LANES + N_EXPERTS].set(b_exp)
    hi = w.astype(BF16)
    lo = (w - hi.astype(F32)).astype(BF16)
    return hi, lo, b


def _segment_mean_matrix(seg):
    r = jnp.arange(MXU_DIM)
    return jnp.where((r[:, None] // seg) == (r[None, :] // seg), 1.0 / seg, 0.0).astype(BF16)


def kernel(x_prompt, x_sample, cache_a_k, cache_a_v, cache_b_k, cache_b_v, g_mix, g_ffn, a_w_qkv, a_g_q, a_g_k, a_lam, a_g_sub, a_w_o, t5_table, kv_g_norm, kv_w, kv_g_k, b_w_q, b_g_q, b_rel, b_w_o, moe_w_grp, moe_b_grp, moe_w_exp, moe_b_exp, moe_w_gate, moe_w_up, moe_w_down):
    bp, seq, d = x_prompt.shape
    db, sd, _ = x_sample.shape
    past = cache_a_k.shape[2]
    depth = g_mix.shape[0]
    n_a = a_w_qkv.shape[0]
    assert n_a == 1 and depth == 2 and d == H_A * 2 * HD_A == H_B * HD_B
    assert seq % ATT_TILE == 0 or seq < ATT_TILE
    w_b = cache_b_k.shape[1]

    xp = x_prompt.reshape(bp * seq, d)
    xs = x_sample.reshape(db * sd, d)
    bd = _segment_mean_matrix(HD_A)
    row = lambda v: v.reshape(1, -1).astype(F32)

    def moe_layer(i, o_p, o_s, wo, xp, xs):
        wr_hi, wr_lo, br = _router_weights(moe_w_grp[i], moe_b_grp[i], moe_w_exp[i], moe_b_exp[i])
        wg, wu, wdn = (moe_w_gate[i].astype(BF16), moe_w_up[i].astype(BF16),
                       moe_w_down[i].astype(BF16))
        outs = []
        for o, x in ((o_p, xp), (o_s, xs)):
            x1, h, comb = _out_router(o, wo, x, row(g_ffn[i]), wr_hi, wr_lo, br)
            outs.append(_moe(h, comb, wg, wu, wdn, x1))
        return outs

    lam_init = 0.8 - 0.6 * math.exp(-0.3 * 0)
    q_scale = (HD_A ** -0.5) * LOG2E
    w_qkv = a_w_qkv[0].astype(BF16)
    gq = jnp.tile(a_g_q[0].reshape(1, 2 * HD_A), (1, H_A)).astype(F32)
    gk = jnp.tile(a_g_k[0].reshape(1, 2 * HD_A), (1, H_A)).astype(F32)
    g0 = row(g_mix[0])
    qp, kp, vp, kpb, vpb = _qkv_a(xp, g0, w_qkv, bd, gq, gk, q_scale)
    qs, ks, vs, ksb, vsb = _qkv_a(xs, g0, w_qkv, bd, gq, gk, q_scale)

    t = min(ATT_TILE, seq)
    r = jnp.arange(t)
    bdiag = _t5_bias(t5_table, r, r)
    bsub = _t5_bias(t5_table, r + t, r)
    cfar = jnp.broadcast_to((t5_table[T5_BUCKETS // 2 - 1] * LOG2E).astype(F32)[:, None, None],
                            (H_A, 1, LANES))
    gsub = row(a_g_sub[0])
    lam_p = a_lam[0].astype(F32)
    op = _flash_a(lam_p, qp, kpb, vpb, bdiag, bsub, cfar, gsub, bp, lam_init)

    qpos_s = past + jnp.arange(sd)
    bias_c = _t5_bias(t5_table, qpos_s, jnp.arange(past))
    bias_n = _t5_bias(t5_table, qpos_s, qpos_s)
    os_ = _sample_a(lam_p, qs, ksb, vsb, cache_a_k[0].reshape(db, past, d),
                    cache_a_v[0].reshape(db, past, d), bias_c, bias_n, gsub, lam_init)

    xp, xs = moe_layer(0, op, os_, a_w_o[0].astype(BF16), xp, xs)

    qb_scale = (HD_B ** -0.5) * LOG2E
    gqb = jnp.tile(b_g_q[0].reshape(1, HD_B), (1, H_B)).astype(F32)
    gkb = jnp.tile(kv_g_k.reshape(1, HD_B), (1, H_B)).astype(F32)
    g1 = row(g_mix[1])
    gkv = row(kv_g_norm)
    wq_b = b_w_q[0].astype(BF16)
    wkv = kv_w.astype(BF16)
    qbp, kbp, vbp, kbpb, vbpb = _proj_b(xp, g1, gkv, wq_b, wkv, bd, gqb, gkb, qb_scale)
    qbs, kbs, vbs, kbsb, vbsb = _proj_b(xs, g1, gkv, wq_b, wkv, bd, gqb, gkb, qb_scale)

    bias_p = _band_bias(b_rel[0], t + r, jnp.arange(2 * t))
    obp = _band_p(qbp, kbpb, vbpb, bias_p, bp)
    kpos_s = jnp.concatenate([past - w_b + jnp.arange(w_b), qpos_s])
    bias_s = _band_bias(b_rel[0], qpos_s, kpos_s)
    obs = _band_s(qbs, kbsb, vbsb, cache_b_k.reshape(db, w_b, d), cache_b_v.reshape(db, w_b, d), bias_s)

    xp, xs = moe_layer(1, obp, obs, b_w_o[0].astype(BF16), xp, xs)

    keep = min(B_PAST, seq)
    kb4 = kbp.reshape(bp, seq, H_B, HD_B)
    vb4 = vbp.reshape(bp, seq, H_B, HD_B)
    return (xp.reshape(bp, seq, d), xs.reshape(db, sd, d),
            kp.reshape(1, bp, seq, H_A, 2 * HD_A), vp.reshape(1, bp, seq, H_A, 2 * HD_A),
            ks.reshape(1, db, sd, H_A, 2 * HD_A), vs.reshape(1, db, sd, H_A, 2 * HD_A),
            kb4[:, seq - keep:], vb4[:, seq - keep:],
            kbs.reshape(db, sd, H_B, HD_B), vbs.reshape(db, sd, H_B, HD_B))
```
